```python
import jax, jax.numpy as jnp
from jax import lax
import numpy as np


D_MODEL = 4096
BATCH = 4
SEQ = 2048
DEPTH = 2

PLE_DIM = 256
BLOCK = 128
DIL_PAIRS = ((128, 1), (512, 4), (2048, 16))
DIL_HEADS = 8
DIL_HEAD_DIM = 128
DIL_WIDTH = len(DIL_PAIRS) * DIL_HEADS * DIL_HEAD_DIM
DIL_OUT = DIL_HEADS * DIL_HEAD_DIM
SWA_HEADS = 64
SWA_KV_HEADS = 8
SWA_HEAD_DIM = 64
SWA_WINDOW = 128
SWA_Q = SWA_HEADS * SWA_HEAD_DIM
SWA_KV = SWA_KV_HEADS * SWA_HEAD_DIM
IN_SIZES = (DIL_WIDTH, DIL_WIDTH, DIL_WIDTH, SWA_Q, SWA_KV, SWA_KV, D_MODEL, D_MODEL)
IN_COLS = sum(IN_SIZES)
N_EXPERTS = 32
TOP_K = 4
D_EXPERT = 768
SWIGLU_LIMIT = 7.0
SWIGLU_ALPHA = 1.702
LN_EPS = 1e-5
DN_ALPHA = (2 * DEPTH) ** 0.25
DN_BETA = (8 * DEPTH) ** -0.25

kernel_name = "hybrid_dilated_swa_sink_moe_deepnorm"


def _alibi_slopes(n):
    return 2.0 ** (-8.0 * jnp.arange(1, n + 1, dtype=jnp.float32) / n)


def _layer_norm(x, g, b):
    xf = x.astype(jnp.float32)
    mu = xf.mean(-1, keepdims=True)
    var = jnp.square(xf - mu).mean(-1, keepdims=True)
    y = (xf - mu) * lax.rsqrt(var + LN_EPS) * g.astype(jnp.float32) + b.astype(jnp.float32)
    return y.astype(x.dtype)


def _banded_attention(q, k, v, slopes, window, inclusive, step, sinks):
    n, L, H, G, hd = q.shape
    nb = -(-L // BLOCK)
    Lp = nb * BLOCK
    pad = Lp - L
    qb = jnp.pad(q, ((0, 0), (0, pad), (0, 0), (0, 0), (0, 0))).reshape(n, nb, BLOCK, H, G, hd)
    kp = jnp.pad(k, ((0, 0), (BLOCK, pad), (0, 0), (0, 0))).reshape(n, nb + 1, BLOCK, H, hd)
    vp = jnp.pad(v, ((0, 0), (BLOCK, pad), (0, 0), (0, 0))).reshape(n, nb + 1, BLOCK, H, hd)
    kb = jnp.concatenate([kp[:, :-1], kp[:, 1:]], axis=2)
    vb = jnp.concatenate([vp[:, :-1], vp[:, 1:]], axis=2)
    scores = jnp.einsum('nbqhgd,nbchd->nbhgqc', qb.astype(jnp.float32), kb.astype(jnp.float32)) * (hd ** -0.5)
    qi = jnp.arange(BLOCK)[:, None]
    ci = jnp.arange(2 * BLOCK)[None, :]
    dist = qi + BLOCK - ci
    kpos = jnp.arange(nb)[:, None, None] * BLOCK + ci[None] - BLOCK
    in_win = (dist <= window) if inclusive else (dist < window)
    valid = (dist >= 0) & in_win & (kpos >= 0)
    bias = -slopes[..., None, None] * (dist * step).astype(jnp.float32)
    scores = jnp.where(valid[None, :, None, None], scores + bias, -jnp.inf)
    m = scores.max(-1)
    if sinks is not None:
        m = jnp.maximum(m, sinks[..., None])
    pexp = jnp.exp(scores - m[..., None])
    l = pexp.sum(-1)
    if sinks is not None:
        l = l + jnp.exp(sinks[..., None] - m)
    o = jnp.einsum('nbhgqc,nbchd->nbqhgd', pexp, vb.astype(jnp.float32))
    o = o / jnp.moveaxis(l, -1, 2)[..., None]
    lse = jnp.moveaxis(m + jnp.log(l), -1, 2)
    o = o.reshape(n, Lp, H, G, hd)[:, :L]
    lse = lse.reshape(n, Lp, H, G)[:, :L]
    return o, lse


def _fold(t, r):
    b, s, h, d = t.shape
    return t.reshape(b, s // r, r, h, d).transpose(0, 2, 1, 3, 4).reshape(b * r, s // r, h, d)


def _unfold(t, r, b):
    L = t.shape[1]
    rest = t.shape[2:]
    t = jnp.moveaxis(t.reshape((b, r, L) + rest), 1, 2)
    return t.reshape((b, L * r) + rest)


def _dilated_attention(q, k, v):
    b = q.shape[0]
    slopes = _alibi_slopes(len(DIL_PAIRS) * DIL_HEADS).reshape(len(DIL_PAIRS), DIL_HEADS, 1)
    outs, lses = [], []
    for g, (w, r) in enumerate(DIL_PAIRS):
        hs = slice(g * DIL_HEADS, (g + 1) * DIL_HEADS)
        qg = _fold(q[:, :, hs], r)[:, :, :, None]
        o, lse = _banded_attention(qg, _fold(k[:, :, hs], r), _fold(v[:, :, hs], r),
                                   slopes[g], w // r, True, r, None)
        outs.append(_unfold(o[:, :, :, 0], r, b))
        lses.append(_unfold(lse[..., 0], r, b))
    o = jnp.stack(outs)
    lse = jnp.stack(lses)
    wts = jax.nn.softmax(lse, axis=0)
    return jnp.einsum('gbsh,gbshd->bshd', wts, o).astype(q.dtype)


def _sink_swa(q, k, v, sinks):
    b, s = q.shape[:2]
    groups = SWA_HEADS // SWA_KV_HEADS
    qg = q.reshape(b, s, SWA_KV_HEADS, groups, SWA_HEAD_DIM)
    slopes = _alibi_slopes(SWA_HEADS).reshape(SWA_KV_HEADS, groups)
    o, _ = _banded_attention(qg, k, v, slopes, SWA_WINDOW, False, 1,
                             sinks.reshape(SWA_KV_HEADS, groups).astype(jnp.float32))
    return o.reshape(b, s, SWA_Q).astype(q.dtype)


def _moe(x, router_w, router_b, w_mlp1, b_mlp1, w_mlp2, b_mlp2):
    b, s, d = x.shape
    xf = x.reshape(b * s, d)
    logits = (xf @ router_w + router_b).astype(jnp.float32)
    top_v, top_i = lax.top_k(logits, TOP_K)
    top_w = jax.nn.softmax(top_v, axis=-1)
    comb = jnp.einsum('nk,nke->ne', top_w, jax.nn.one_hot(top_i, N_EXPERTS, dtype=jnp.float32))
    y = jnp.zeros((b * s, d), jnp.float32)
    for e in range(N_EXPERTS):
        h = xf @ w_mlp1[e] + b_mlp1[e]
        glu = jnp.minimum(h[:, :D_EXPERT], SWIGLU_LIMIT)
        lin = jnp.clip(h[:, D_EXPERT:], -SWIGLU_LIMIT, SWIGLU_LIMIT)
        a = glu * jax.nn.sigmoid(SWIGLU_ALPHA * glu) * (lin + 1.0)
        y = y + comb[:, e:e + 1] * (a @ w_mlp2[e] + b_mlp2[e]).astype(jnp.float32)
    return y.reshape(b, s, d).astype(x.dtype)


def setup_inputs(seed: int = 0) -> dict:
    key = jax.random.key(seed)
    ks = jax.random.split(key, 20)

    def nrm(k, shape, scale):
        return jax.random.normal(k, shape, jnp.float32) * scale

    return {
        "x": nrm(ks[0], (BATCH, SEQ, D_MODEL), 1.0),
        "p": nrm(ks[1], (DEPTH, BATCH, SEQ, PLE_DIM), 1.0),
        "w_in": nrm(ks[2], (DEPTH, D_MODEL, IN_COLS), D_MODEL ** -0.5),
        "w_proj_a": nrm(ks[3], (DEPTH, DIL_OUT, D_MODEL), DN_BETA * DIL_OUT ** -0.5),
        "w_proj_b": nrm(ks[4], (DEPTH, SWA_Q, D_MODEL), DN_BETA * SWA_Q ** -0.5),
        "w_o": nrm(ks[5], (DEPTH, D_MODEL, D_MODEL), DN_BETA * D_MODEL ** -0.5),
        "sinks": nrm(ks[6], (DEPTH, SWA_HEADS), 0.5),
        "ln1_g": 1.0 + nrm(ks[7], (DEPTH, D_MODEL), 0.02),
        "ln1_b": nrm(ks[8], (DEPTH, D_MODEL), 0.02),
        "router_w": nrm(ks[9], (DEPTH, D_MODEL, N_EXPERTS), D_MODEL ** -0.5),
        "router_b": nrm(ks[10], (DEPTH, N_EXPERTS), 0.01),
        "w_mlp1": nrm(ks[11], (DEPTH, N_EXPERTS, D_MODEL, 2 * D_EXPERT), D_MODEL ** -0.5),
        "b_mlp1": nrm(ks[12], (DEPTH, N_EXPERTS, 2 * D_EXPERT), 0.01),
        "w_mlp2": nrm(ks[13], (DEPTH, N_EXPERTS, D_EXPERT, D_MODEL), DN_BETA * D_EXPERT ** -0.5),
        "b_mlp2": nrm(ks[14], (DEPTH, N_EXPERTS, D_MODEL), 0.01),
        "w_ple_up": nrm(ks[15], (DEPTH, PLE_DIM, D_MODEL), DN_BETA * PLE_DIM ** -0.5),
        "w_ple_gate": nrm(ks[16], (DEPTH, D_MODEL, D_MODEL), D_MODEL ** -0.5),
        "ln2_g": 1.0 + nrm(ks[17], (DEPTH, D_MODEL), 0.02),
        "ln2_b": nrm(ks[18], (DEPTH, D_MODEL), 0.02),
    }


def reference(x, p, w_in, w_proj_a, w_proj_b, w_o, sinks, ln1_g, ln1_b, router_w, router_b,
              w_mlp1, b_mlp1, w_mlp2, b_mlp2, w_ple_up, w_ple_gate, ln2_g, ln2_b):
    b, s, _ = x.shape
    cuts = np.cumsum(IN_SIZES)[:-1].tolist()
    nd = len(DIL_PAIRS) * DIL_HEADS
    for i in range(DEPTH):
        proj = x @ w_in[i]
        qa, ka, va, qb, kb, vb, ga, gb = jnp.split(proj, cuts, axis=-1)
        ya = _dilated_attention(qa.reshape(b, s, nd, DIL_HEAD_DIM),
                                ka.reshape(b, s, nd, DIL_HEAD_DIM),
                                va.reshape(b, s, nd, DIL_HEAD_DIM)).reshape(b, s, DIL_OUT) @ w_proj_a[i]
        yb = _sink_swa(qb.reshape(b, s, SWA_HEADS, SWA_HEAD_DIM),
                       kb.reshape(b, s, SWA_KV_HEADS, SWA_HEAD_DIM),
                       vb.reshape(b, s, SWA_KV_HEADS, SWA_HEAD_DIM), sinks[i]) @ w_proj_b[i]
        mixed = (jax.nn.sigmoid(ga) * ya + jax.nn.sigmoid(gb) * yb) @ w_o[i]
        x = _layer_norm(DN_ALPHA * x + mixed, ln1_g[i], ln1_b[i])
        ffn = _moe(x, router_w[i], router_b[i], w_mlp1[i], b_mlp1[i], w_mlp2[i], b_mlp2[i])
        ple = jax.nn.sigmoid(x @ w_ple_gate[i]) * (p[i] @ w_ple_up[i])
        x = _layer_norm(DN_ALPHA * x + ffn + ple, ln2_g[i], ln2_b[i])
    return x
```

```python
import functools

import jax
import jax.numpy as jnp
from jax import lax
from jax.experimental import pallas as pl
from jax.experimental.pallas import tpu as pltpu

BLOCK = 128
DIL_PAIRS = ((128, 1), (512, 4), (2048, 16))
DIL_HEADS = 8
DIL_HEAD_DIM = 128
DIL_GROUP_COLS = DIL_HEADS * DIL_HEAD_DIM
DIL_WIDTH = len(DIL_PAIRS) * DIL_GROUP_COLS
SWA_HEADS = 64
SWA_KV_HEADS = 8
SWA_HEAD_DIM = 64
SWA_WINDOW = 128
SWA_Q = SWA_HEADS * SWA_HEAD_DIM
SWA_KV = SWA_KV_HEADS * SWA_HEAD_DIM
N_EXPERTS = 32
TOP_K = 4
SWIGLU_LIMIT = 7.0
SWIGLU_ALPHA = 1.702
LN_EPS = 1e-5

QA0, KA0, VA0 = 0, DIL_WIDTH, 2 * DIL_WIDTH
QB0 = 3 * DIL_WIDTH
KB0 = QB0 + SWA_Q
VB0 = KB0 + SWA_KV
GA0 = VB0 + SWA_KV

LANES = 128
V7X_VMEM_LIMIT_BYTES = 56 * 1024 * 1024
VMEM_COMPILER_SCRATCH_BYTES = 6 * 1024 * 1024

MM_TM, MM_TN = 1024, 512
ROW_TILE = 256
EXPERT_TM = 256
EXPERT_SPLIT = 2

F32 = jnp.float32
BF16 = jnp.bfloat16
NEG_INF = float("-inf")


def _vmem_limit(block_bytes, scratch_bytes=0):
    need = 2 * sum(block_bytes) + scratch_bytes + VMEM_COMPILER_SCRATCH_BYTES
    return int(min(max(need, 16 * 1024 * 1024), V7X_VMEM_LIMIT_BYTES))


def _nbytes(shape, dtype):
    n = 1
    for s in shape:
        if s is not None:
            n *= s
    return n * jnp.dtype(dtype).itemsize


def _sigmoid(x):
    return 1.0 / (1.0 + jnp.exp(-x))


def _mm_body(x_ref, w_ref, *rest, n_extra, epilogue, cast_rows):
    extra_refs = rest[:n_extra]
    o_ref = rest[n_extra]
    wbf_ref = rest[n_extra + 1]

    @pl.when(pl.program_id(1) == 0)
    def _cast():
        def body(c, carry):
            r = pl.multiple_of(c * cast_rows, cast_rows)
            wbf_ref[pl.ds(r, cast_rows), :] = w_ref[pl.ds(r, cast_rows), :].astype(BF16)
            return carry
        lax.fori_loop(0, w_ref.shape[0] // cast_rows, body, 0)

    acc = jnp.dot(x_ref[...], wbf_ref[...], preferred_element_type=F32)
    epilogue(acc, pl.program_id(0), o_ref, *extra_refs)


def _matmul(x, w, layer, *, extras=(), epilogue, out_dtype, name):
    m, k = x.shape
    n = w.shape[-1]
    tm, tn = min(MM_TM, m), min(MM_TN, n)
    cast_rows = min(512, k)
    in_specs = [
        pl.BlockSpec((tm, k), lambda j, i: (i, 0)),
        pl.BlockSpec((None, k, tn), lambda j, i: (layer, 0, j)),
    ]
    blocks = [_nbytes((tm, k), x.dtype), _nbytes((k, tn), w.dtype), _nbytes((tm, tn), out_dtype)]
    arrays = [x, w]
    for arr, bshape, imap in extras:
        in_specs.append(pl.BlockSpec(bshape, imap))
        blocks.append(_nbytes(bshape, arr.dtype))
        arrays.append(arr)
    return pl.pallas_call(
        functools.partial(_mm_body, n_extra=len(extras), epilogue=epilogue, cast_rows=cast_rows),
        out_shape=jax.ShapeDtypeStruct((m, n), out_dtype),
        grid=(n // tn, m // tm),
        in_specs=in_specs,
        out_specs=pl.BlockSpec((tm, tn), lambda j, i: (i, j)),
        scratch_shapes=[pltpu.VMEM((k, tn), BF16)],
        compiler_params=pltpu.CompilerParams(
            dimension_semantics=("arbitrary", "arbitrary"),
            vmem_limit_bytes=_vmem_limit(blocks, _nbytes((k, tn), BF16) + _nbytes((tm, tn), F32)),
        ),
        name=name,
    )(*arrays)


def _band_geometry(first_block, inclusive):
    row = lax.broadcasted_iota(jnp.int32, (BLOCK, 2 * BLOCK), 0)
    col = lax.broadcasted_iota(jnp.int32, (BLOCK, 2 * BLOCK), 1)
    dist = row + BLOCK - col
    in_win = (dist <= BLOCK) if inclusive else (dist < BLOCK)
    valid = (dist >= 0) & in_win & (col >= BLOCK * first_block.astype(jnp.int32))
    return dist.astype(F32), valid


def _dil_body(q_ref, kp_ref, kc_ref, vp_ref, vc_ref, o_ref, lse_ref, *, slopes, step):
    distf, valid = _band_geometry(pl.program_id(1) == 0, True)
    scale = DIL_HEAD_DIM ** -0.5
    for h in range(DIL_HEADS):
        cs = slice(h * DIL_HEAD_DIM, (h + 1) * DIL_HEAD_DIM)
        kcat = jnp.concatenate([kp_ref[:, cs], kc_ref[:, cs]], axis=0)
        vcat = jnp.concatenate([vp_ref[:, cs], vc_ref[:, cs]], axis=0)
        s = lax.dot_general(q_ref[:, cs], kcat, (((1,), (1,)), ((), ())),
                            preferred_element_type=F32)
        s = s * scale - (slopes[h] * step) * distf
        s = jnp.where(valid, s, NEG_INF)
        m = jnp.max(s, axis=-1, keepdims=True)
        p = jnp.exp(s - m)
        l = jnp.sum(p, axis=-1, keepdims=True)
        o = jnp.dot(p.astype(BF16), vcat, preferred_element_type=F32)
        o_ref[:, cs] = (o / l).astype(o_ref.dtype)
        lse_ref[:, cs] = jnp.broadcast_to(m + jnp.log(l), (BLOCK, DIL_HEAD_DIM))


def _dilated_group(proj3, g, in_cols):
    b, s, _ = proj3.shape
    window, r = DIL_PAIRS[g]
    assert window // r == BLOCK and s % (r * BLOCK) == 0 and in_cols % DIL_GROUP_COLS == 0
    cb = in_cols // DIL_GROUP_COLS
    view = proj3.reshape(b, s // r, r * in_cols)
    n_heads = len(DIL_PAIRS) * DIL_HEADS
    slopes = tuple(2.0 ** (-8.0 * (g * DIL_HEADS + h + 1) / n_heads) for h in range(DIL_HEADS))
    blk = (None, BLOCK, DIL_GROUP_COLS)

    def cur(c0):
        return pl.BlockSpec(blk, lambda bi, i, rho: (bi, i, rho * cb + c0 + g))

    def prev(c0):
        return pl.BlockSpec(blk, lambda bi, i, rho: (bi, jnp.maximum(i - 1, 0), rho * cb + c0 + g))

    ng = len(DIL_PAIRS)
    out_spec = pl.BlockSpec(blk, lambda bi, i, rho: (bi, i, rho))
    tile = _nbytes(blk, BF16)
    o, lse = pl.pallas_call(
        functools.partial(_dil_body, slopes=slopes, step=float(r)),
        out_shape=(jax.ShapeDtypeStruct((b, s // r, r * DIL_GROUP_COLS), BF16),
                   jax.ShapeDtypeStruct((b, s // r, r * DIL_GROUP_COLS), F32)),
        grid=(b, s // (r * BLOCK), r),
        in_specs=[cur(0), prev(ng), cur(ng), prev(2 * ng), cur(2 * ng)],
        out_specs=(out_spec, out_spec),
        compiler_params=pltpu.CompilerParams(
            dimension_semantics=("arbitrary", "arbitrary", "arbitrary"),
            vmem_limit_bytes=_vmem_limit([tile] * 6 + [2 * tile]),
        ),
        name=f"dilated_attn_g{g}",
    )(view, view, view, view, view)
    return o.reshape(b * s, DIL_GROUP_COLS), lse.reshape(b * s, DIL_GROUP_COLS)


def _dil_merge_body(o0, o1, o2, l0, l1, l2, out_ref):
    a0, a1, a2 = l0[...], l1[...], l2[...]
    m = jnp.maximum(jnp.maximum(a0, a1), a2)
    e0, e1, e2 = jnp.exp(a0 - m), jnp.exp(a1 - m), jnp.exp(a2 - m)
    num = e0 * o0[...].astype(F32) + e1 * o1[...].astype(F32) + e2 * o2[...].astype(F32)
    out_ref[...] = (num / (e0 + e1 + e2)).astype(out_ref.dtype)


def _dil_merge(os_, lses):
    n = os_[0].shape[0]
    tr = min(512, n)
    spec = pl.BlockSpec((tr, DIL_GROUP_COLS), lambda i: (i, 0))
    return pl.pallas_call(
        _dil_merge_body,
        out_shape=jax.ShapeDtypeStruct((n, DIL_GROUP_COLS), BF16),
        grid=(n // tr,),
        in_specs=[spec] * 6,
        out_specs=spec,
        compiler_params=pltpu.CompilerParams(dimension_semantics=("arbitrary",)),
        name="dilated_merge",
    )(*os_, *lses)


SWA_GROUP = SWA_HEADS // SWA_KV_HEADS
SWA_PAIR_COLS = 2 * SWA_GROUP * SWA_HEAD_DIM


def _swa_body(slopes_ref, sinks_ref, q_ref, kp_ref, kc_ref, vp_ref, vc_ref, o_ref):
    pair = pl.program_id(2)
    distf, valid = _band_geometry(pl.program_id(1) == 0, False)
    lane = lax.broadcasted_iota(jnp.int32, (BLOCK, LANES), 1)
    low = lane < SWA_HEAD_DIM
    low_bf = lane.astype(F32).astype(BF16) < SWA_HEAD_DIM
    scale = SWA_HEAD_DIM ** -0.5
    kcat = jnp.concatenate([kp_ref[...], kc_ref[...]], axis=0)
    vcat = jnp.concatenate([vp_ref[...], vc_ref[...]], axis=0)
    zero = jnp.zeros((BLOCK, LANES), BF16)
    for g in range(2):
        ks = kcat[:, g * SWA_HEAD_DIM:(g + 1) * SWA_HEAD_DIM]
        vs = vcat[:, g * SWA_HEAD_DIM:(g + 1) * SWA_HEAD_DIM]
        kd = jnp.concatenate([ks, ks], axis=1)
        vd = jnp.concatenate([vs, vs], axis=1)
        qs = []
        for sl in range(SWA_GROUP // 2):
            c0 = (g * (SWA_GROUP // 2) + sl) * LANES
            q2 = q_ref[:, c0:c0 + LANES]
            qs.append(jnp.where(low_bf, q2, zero))
            qs.append(jnp.where(low_bf, zero, q2))
        qst = jnp.concatenate(qs, axis=0)
        s_all = lax.dot_general(qst, kd, (((1,), (1,)), ((), ())),
                                preferred_element_type=F32)
        ps, inv = [], []
        for hh in range(SWA_GROUP):
            head = pair * (2 * SWA_GROUP) + g * SWA_GROUP + hh
            slope = slopes_ref[head]
            sink = sinks_ref[head]
            s = s_all[hh * BLOCK:(hh + 1) * BLOCK] * scale - slope * distf
            s = jnp.where(valid, s, NEG_INF)
            m = jnp.maximum(jnp.max(s, axis=-1, keepdims=True), sink)
            p = jnp.exp(s - m)
            l = jnp.sum(p, axis=-1, keepdims=True) + jnp.exp(sink - m)
            ps.append(p.astype(BF16))
            inv.append(1.0 / l)
        o_all = jnp.dot(jnp.concatenate(ps, axis=0), vd, preferred_element_type=F32)
        for sl in range(SWA_GROUP // 2):
            c0 = (g * (SWA_GROUP // 2) + sl) * LANES
            oe = o_all[(2 * sl) * BLOCK:(2 * sl + 1) * BLOCK] * inv[2 * sl]
            oo = o_all[(2 * sl + 1) * BLOCK:(2 * sl + 2) * BLOCK] * inv[2 * sl + 1]
            o_ref[:, c0:c0 + LANES] = jnp.where(low, oe, oo).astype(o_ref.dtype)


def _swa(proj3, sinks):
    b, s, _ = proj3.shape
    assert s % BLOCK == 0
    slopes = 2.0 ** (-8.0 * jnp.arange(1, SWA_HEADS + 1, dtype=F32) / SWA_HEADS)
    qc0, kc0, vc0 = QB0 // SWA_PAIR_COLS, KB0 // LANES, VB0 // LANES
    n_pairs = SWA_KV_HEADS // 2
    qblk = (None, BLOCK, SWA_PAIR_COLS)
    kblk = (None, BLOCK, LANES)

    def cur(c0):
        return pl.BlockSpec(kblk, lambda bi, i, p, *_: (bi, i, c0 + p))

    def prev(c0):
        return pl.BlockSpec(kblk, lambda bi, i, p, *_: (bi, jnp.maximum(i - 1, 0), c0 + p))

    out = pl.pallas_call(
        _swa_body,
        out_shape=jax.ShapeDtypeStruct((b, s, SWA_Q), BF16),
        grid_spec=pltpu.PrefetchScalarGridSpec(
            num_scalar_prefetch=2,
            grid=(b, s // BLOCK, n_pairs),
            in_specs=[pl.BlockSpec(qblk, lambda bi, i, p, *_: (bi, i, qc0 + p)),
                      prev(kc0), cur(kc0), prev(vc0), cur(vc0)],
            out_specs=pl.BlockSpec(qblk, lambda bi, i, p, *_: (bi, i, p)),
        ),
        compiler_params=pltpu.CompilerParams(
            dimension_semantics=("arbitrary", "arbitrary", "arbitrary"),
            vmem_limit_bytes=_vmem_limit([_nbytes(qblk, BF16)] * 2 + [_nbytes(kblk, BF16)] * 4,
                                         4 * _nbytes((SWA_GROUP * BLOCK, 2 * BLOCK), F32)),
        ),
        name="swa_attn",
    )(slopes, sinks.astype(F32), proj3, proj3, proj3, proj3, proj3)
    return out.reshape(b * s, SWA_Q)


def _layer_norm_rows(h, g, b):
    mu = jnp.mean(h, axis=-1, keepdims=True)
    d = h - mu
    var = jnp.mean(d * d, axis=-1, keepdims=True)
    return d * lax.rsqrt(var + LN_EPS) * g + b


META_IDX, META_W, META_RANK = 0, TOP_K, 2 * TOP_K


def _ln_router_body(h_ref, g_ref, b_ref, rw_ref, rb_ref, x_ref, xb_ref, meta_ref, cnt_ref, carry_ref):
    i = pl.program_id(0)
    tr = h_ref.shape[0]

    @pl.when(i == 0)
    def _init():
        carry_ref[...] = jnp.zeros_like(carry_ref)

    x = _layer_norm_rows(h_ref[...], g_ref[...], b_ref[...])
    x_ref[...] = x
    xb_ref[...] = x.astype(BF16)

    lane = lax.broadcasted_iota(jnp.int32, (tr, LANES), 1)
    lanef = lane.astype(F32)
    logits = jnp.dot(x, rw_ref[...], preferred_element_type=F32,
                     precision=lax.Precision.HIGHEST) + rb_ref[...]
    work = jnp.where(lane < N_EXPERTS, logits, NEG_INF)
    masks, vals, idxs = [], [], []
    for _ in range(TOP_K):
        v = jnp.max(work, axis=-1, keepdims=True)
        idx = jnp.min(jnp.where(work == v, lanef, float(LANES)), axis=-1, keepdims=True)
        sel = lanef == idx
        masks.append(sel)
        vals.append(v)
        idxs.append(idx)
        work = jnp.where(sel, NEG_INF, work)
    es = [jnp.exp(v - vals[0]) for v in vals]
    denom = es[0] + es[1] + es[2] + es[3]

    onehot = jnp.zeros((tr, LANES), F32)
    for sel in masks:
        onehot = jnp.where(sel, 1.0, onehot)
    r_i = lax.broadcasted_iota(jnp.int32, (tr, tr), 0)
    c_i = lax.broadcasted_iota(jnp.int32, (tr, tr), 1)
    tri = jnp.where(r_i > c_i, 1.0, 0.0).astype(BF16)
    rank = jnp.dot(tri, onehot.astype(BF16), preferred_element_type=F32) + carry_ref[...]
    carry_ref[...] = carry_ref[...] + jnp.sum(onehot, axis=0, keepdims=True)

    meta = jnp.zeros((tr, LANES), F32)
    for k in range(TOP_K):
        rk = jnp.sum(jnp.where(masks[k], rank, 0.0), axis=-1, keepdims=True)
        meta = jnp.where(lane == META_IDX + k, idxs[k], meta)
        meta = jnp.where(lane == META_W + k, es[k] / denom, meta)
        meta = jnp.where(lane == META_RANK + k, rk, meta)
    meta_ref[...] = meta

    @pl.when(i == pl.num_programs(0) - 1)
    def _fin():
        cnt_ref[...] = jnp.broadcast_to(carry_ref[...], cnt_ref.shape)


def _ln_router(h, ln_g, ln_b, router_w, router_b, layer):
    n, d = h.shape
    tr = min(ROW_TILE, n)
    rw = jnp.zeros((d, LANES), F32).at[:, :N_EXPERTS].set(router_w[layer])
    rb = jnp.zeros((1, LANES), F32).at[0, :N_EXPERTS].set(router_b[layer])
    row = pl.BlockSpec((tr, d), lambda i: (i, 0))
    vec = pl.BlockSpec((None, 1, d), lambda i: (layer, 0, 0))
    return pl.pallas_call(
        _ln_router_body,
        out_shape=(jax.ShapeDtypeStruct((n, d), F32), jax.ShapeDtypeStruct((n, d), BF16),
                   jax.ShapeDtypeStruct((n, LANES), F32), jax.ShapeDtypeStruct((8, LANES), F32)),
        grid=(n // tr,),
        in_specs=[row, vec, vec, pl.BlockSpec((d, LANES), lambda i: (0, 0)),
                  pl.BlockSpec((1, LANES), lambda i: (0, 0))],
        out_specs=(row, row, pl.BlockSpec((tr, LANES), lambda i: (i, 0)),
                   pl.BlockSpec((8, LANES), lambda i: (0, 0))),
        scratch_shapes=[pltpu.VMEM((1, LANES), F32)],
        compiler_params=pltpu.CompilerParams(
            dimension_semantics=("arbitrary",),
            vmem_limit_bytes=_vmem_limit([_nbytes((tr, d), F32)] * 2 + [_nbytes((tr, d), BF16),
                                                                       _nbytes((d, LANES), F32)],
                                         2 * _nbytes((tr, d), F32)),
        ),
        name="ln1_router",
    )(h, ln_g.reshape(ln_g.shape[0], 1, d), ln_b.reshape(ln_b.shape[0], 1, d), rw, rb)


def _route_tables(meta, cnt, n_tiles_max):
    idx = meta[:, META_IDX:META_IDX + TOP_K].astype(jnp.int32)
    w4 = meta[:, META_W:META_W + TOP_K]
    rank = meta[:, META_RANK:META_RANK + TOP_K].astype(jnp.int32)
    counts = cnt[0, :N_EXPERTS].astype(jnp.int32)
    tiles_e = (counts + EXPERT_TM - 1) // EXPERT_TM
    tile_end = jnp.cumsum(tiles_e)
    tile_start = tile_end - tiles_e
    pos = tile_start[idx] * EXPERT_TM + rank
    n_rows = n_tiles_max * EXPERT_TM
    tok = jnp.broadcast_to(jnp.arange(idx.shape[0], dtype=jnp.int32)[:, None], idx.shape)
    row_src = jnp.zeros((n_rows,), jnp.int32).at[pos.reshape(-1)].set(tok.reshape(-1))

    n_items = n_tiles_max * EXPERT_SPLIT
    s = jnp.arange(n_items, dtype=jnp.int32)
    item_end = tile_end * EXPERT_SPLIT
    n_valid = item_end[-1]
    valid = s < n_valid
    s_eff = jnp.minimum(s, n_valid - 1)
    e = jnp.searchsorted(item_end, s_eff, side="right").astype(jnp.int32)
    e = jnp.minimum(e, N_EXPERTS - 1)
    local = s_eff - tile_start[e] * EXPERT_SPLIT
    te = jnp.maximum(tiles_e[e], 1)
    wchunk = local // te
    key = e * EXPERT_SPLIT + wchunk
    first = jnp.concatenate([jnp.ones((1,), jnp.int32), (key[1:] != key[:-1]).astype(jnp.int32)])
    idle = s - n_valid
    tile = jnp.where(valid, tile_start[e] + local % te, tile_end[-1] + idle // EXPERT_SPLIT)
    chunk = jnp.where(valid, wchunk, idle % EXPERT_SPLIT)
    valid = valid.astype(jnp.int32)
    return pos, w4, row_src, (tile, chunk, wchunk, e, first * valid, valid)


def _gather_rows_body(src_ref, x_hbm, o_ref, buf_ref, sem):
    tg = buf_ref.shape[0]

    def row_copy(r):
        src = src_ref[0, 0, r]
        return pltpu.make_async_copy(x_hbm.at[pl.ds(src, 1), :], buf_ref.at[pl.ds(r, 1), :], sem)

    def start(r, carry):
        row_copy(r).start()
        return carry

    def wait(r, carry):
        row_copy(r).wait()
        return carry

    lax.fori_loop(0, tg, start, 0)
    lax.fori_loop(0, tg, wait, 0)
    o_ref[...] = buf_ref[...].astype(o_ref.dtype)


def _gather_rows(x, row_src):
    n_rows = row_src.shape[0]
    d = x.shape[1]
    tg = EXPERT_TM
    return pl.pallas_call(
        _gather_rows_body,
        out_shape=jax.ShapeDtypeStruct((n_rows, d), BF16),
        grid=(n_rows // tg,),
        in_specs=[pl.BlockSpec((1, 1, tg), lambda t: (t, 0, 0), memory_space=pltpu.SMEM),
                  pl.BlockSpec(memory_space=pl.ANY)],
        out_specs=pl.BlockSpec((tg, d), lambda t: (t, 0)),
        scratch_shapes=[pltpu.VMEM((tg, d), x.dtype), pltpu.SemaphoreType.DMA],
        compiler_params=pltpu.CompilerParams(
            dimension_semantics=("arbitrary",),
            vmem_limit_bytes=_vmem_limit([_nbytes((tg, d), BF16)], 2 * _nbytes((tg, d), x.dtype)),
        ),
        name="moe_gather",
    )(row_src.reshape(n_rows // tg, 1, tg), x)


def _cast_weight(w_ref, wbf_ref, rows):
    def body(c, carry):
        r = pl.multiple_of(c * rows, rows)
        wbf_ref[pl.ds(r, rows), :] = w_ref[pl.ds(r, rows), :].astype(BF16)
        return carry
    lax.fori_loop(0, w_ref.shape[0] // rows, body, 0)


def _expert_up_body(tile_ref, chunk_ref, wchunk_ref, e_ref, first_ref, valid_ref,
                    xs_ref, wg_ref, wl_ref, bg_ref, bl_ref, a_ref, wgbf_ref, wlbf_ref):
    s = pl.program_id(0)

    @pl.when(first_ref[s] == 1)
    def _cast():
        rows = min(512, wg_ref.shape[0])
        _cast_weight(wg_ref, wgbf_ref, rows)
        _cast_weight(wl_ref, wlbf_ref, rows)

    @pl.when(valid_ref[s] == 0)
    def _idle():
        a_ref[...] = jnp.zeros_like(a_ref)

    @pl.when(valid_ref[s] == 1)
    def _compute():
        xs = xs_ref[...]
        hg = jnp.dot(xs, wgbf_ref[...], preferred_element_type=F32) + bg_ref[...]
        hl = jnp.dot(xs, wlbf_ref[...], preferred_element_type=F32) + bl_ref[...]
        glu = jnp.minimum(hg, SWIGLU_LIMIT)
        lin = jnp.clip(hl, -SWIGLU_LIMIT, SWIGLU_LIMIT)
        a_ref[...] = (glu * _sigmoid(SWIGLU_ALPHA * glu) * (lin + 1.0)).astype(a_ref.dtype)


def _expert_up(xs, w_mlp1, b_mlp1, layer, items):
    n_rows, d = xs.shape
    de = w_mlp1.shape[-1] // 2
    ch = de // EXPERT_SPLIT
    assert ch % LANES == 0
    depth, ne = b_mlp1.shape[:2]
    b1 = b_mlp1.reshape(depth, ne, 1, 2 * de)
    n_items = items[0].shape[0]
    wblk = (None, None, d, ch)
    bblk = (None, None, 1, ch)
    in_specs = [
        pl.BlockSpec((EXPERT_TM, d), lambda s, tl, ck, wc, e, f, v: (tl[s], 0)),
        pl.BlockSpec(wblk, lambda s, tl, ck, wc, e, f, v: (layer, e[s], 0, wc[s])),
        pl.BlockSpec(wblk, lambda s, tl, ck, wc, e, f, v: (layer, e[s], 0, EXPERT_SPLIT + wc[s])),
        pl.BlockSpec(bblk, lambda s, tl, ck, wc, e, f, v: (layer, e[s], 0, wc[s])),
        pl.BlockSpec(bblk, lambda s, tl, ck, wc, e, f, v: (layer, e[s], 0, EXPERT_SPLIT + wc[s])),
    ]
    return pl.pallas_call(
        _expert_up_body,
        out_shape=jax.ShapeDtypeStruct((n_rows, de), BF16),
        grid_spec=pltpu.PrefetchScalarGridSpec(
            num_scalar_prefetch=6,
            grid=(n_items,),
            in_specs=in_specs,
            out_specs=pl.BlockSpec((EXPERT_TM, ch), lambda s, tl, ck, wc, e, f, v: (tl[s], ck[s])),
            scratch_shapes=[pltpu.VMEM((d, ch), BF16), pltpu.VMEM((d, ch), BF16)],
        ),
        compiler_params=pltpu.CompilerParams(
            dimension_semantics=("arbitrary",),
            vmem_limit_bytes=_vmem_limit([_nbytes((EXPERT_TM, d), BF16), 2 * _nbytes((d, ch), F32),
                                          _nbytes((EXPERT_TM, ch), BF16)], 2 * _nbytes((d, ch), BF16)),
        ),
        name="moe_expert_up",
    )(*items, xs, w_mlp1, w_mlp1, b1, b1)


def _expert_down_body(tile_ref, chunk_ref, wchunk_ref, e_ref, first_ref, valid_ref,
                      a_ref, w_ref, b_ref, y_ref, wbf_ref):
    s = pl.program_id(0)

    @pl.when(first_ref[s] == 1)
    def _cast():
        _cast_weight(w_ref, wbf_ref, min(256, w_ref.shape[0]))

    @pl.when(valid_ref[s] == 0)
    def _idle():
        y_ref[...] = jnp.zeros_like(y_ref)

    @pl.when(valid_ref[s] == 1)
    def _compute():
        y_ref[...] = (jnp.dot(a_ref[...], wbf_ref[...], preferred_element_type=F32)
                      + b_ref[...]).astype(y_ref.dtype)


def _expert_down(a, w_mlp2, b_mlp2, layer, items):
    n_rows, de = a.shape
    d = w_mlp2.shape[-1]
    ch = d // EXPERT_SPLIT
    depth, ne = b_mlp2.shape[:2]
    b2 = b_mlp2.reshape(depth, ne, 1, d)
    n_items = items[0].shape[0]
    in_specs = [
        pl.BlockSpec((EXPERT_TM, de), lambda s, tl, ck, wc, e, f, v: (tl[s], 0)),
        pl.BlockSpec((None, None, de, ch), lambda s, tl, ck, wc, e, f, v: (layer, e[s], 0, wc[s])),
        pl.BlockSpec((None, None, 1, ch), lambda s, tl, ck, wc, e, f, v: (layer, e[s], 0, wc[s])),
    ]
    return pl.pallas_call(
        _expert_down_body,
        out_shape=jax.ShapeDtypeStruct((n_rows, d), F32),
        grid_spec=pltpu.PrefetchScalarGridSpec(
            num_scalar_prefetch=6,
            grid=(n_items,),
            in_specs=in_specs,
            out_specs=pl.BlockSpec((EXPERT_TM, ch), lambda s, tl, ck, wc, e, f, v: (tl[s], ck[s])),
            scratch_shapes=[pltpu.VMEM((de, ch), BF16)],
        ),
        compiler_params=pltpu.CompilerParams(
            dimension_semantics=("arbitrary",),
            vmem_limit_bytes=_vmem_limit([_nbytes((EXPERT_TM, de), BF16), _nbytes((de, ch), F32),
                                          _nbytes((EXPERT_TM, ch), F32)], _nbytes((de, ch), BF16)),
        ),
        name="moe_expert_down",
    )(*items, a, w_mlp2, b2)


def _combine_ln_body(pos_ref, x_ref, ple_ref, meta_ref, g_ref, b_ref, y_hbm,
                     o_ref, ob_ref, buf_ref, sem, *, alpha):
    tr = x_ref.shape[0]

    def row_copy(r, k):
        src = pos_ref[0, 0, r * TOP_K + k]
        return pltpu.make_async_copy(y_hbm.at[pl.ds(src, 1), :], buf_ref.at[k, pl.ds(r, 1), :], sem)

    def start(r, carry):
        for k in range(TOP_K):
            row_copy(r, k).start()
        return carry

    def wait(r, carry):
        for k in range(TOP_K):
            row_copy(r, k).wait()
        return carry

    lax.fori_loop(0, tr, start, 0)
    h = alpha * x_ref[...] + ple_ref[...].astype(F32)
    lax.fori_loop(0, tr, wait, 0)
    meta = meta_ref[...]
    for k in range(TOP_K):
        h = h + meta[:, META_W + k:META_W + k + 1] * buf_ref[k]
    x = _layer_norm_rows(h, g_ref[...], b_ref[...])
    o_ref[...] = x
    ob_ref[...] = x.astype(BF16)


def _combine_ln(x1, ple, meta, pos, y_sorted, ln_g, ln_b, layer, alpha):
    n, d = x1.shape
    tr = min(ROW_TILE, n)
    row = pl.BlockSpec((tr, d), lambda t: (t, 0))
    vec = pl.BlockSpec((None, 1, d), lambda t: (layer, 0, 0))
    return pl.pallas_call(
        functools.partial(_combine_ln_body, alpha=alpha),
        out_shape=(jax.ShapeDtypeStruct((n, d), F32), jax.ShapeDtypeStruct((n, d), BF16)),
        grid=(n // tr,),
        in_specs=[pl.BlockSpec((1, 1, tr * TOP_K), lambda t: (t, 0, 0), memory_space=pltpu.SMEM),
                  row, row, pl.BlockSpec((tr, LANES), lambda t: (t, 0)), vec, vec,
                  pl.BlockSpec(memory_space=pl.ANY)],
        out_specs=(row, row),
        scratch_shapes=[pltpu.VMEM((TOP_K, tr, d), F32), pltpu.SemaphoreType.DMA],
        compiler_params=pltpu.CompilerParams(
            dimension_semantics=("arbitrary",),
            vmem_limit_bytes=_vmem_limit([_nbytes((tr, d), F32)] * 2 + [_nbytes((tr, d), BF16)] * 2,
                                         (TOP_K + 2) * _nbytes((tr, d), F32)),
        ),
        name="moe_combine_ln2",
    )(pos.reshape(n // tr, 1, tr * TOP_K), x1, ple, meta, ln_g.reshape(ln_g.shape[0], 1, d),
      ln_b.reshape(ln_b.shape[0], 1, d), y_sorted)


def _layer(x, xb, p_l, layer, alpha, params):
    (w_in, w_proj_a, w_proj_b, w_o, sinks, ln1_g, ln1_b, router_w, router_b,
     w_mlp1, b_mlp1, w_mlp2, b_mlp2, w_ple_up, w_ple_gate, ln2_g, ln2_b, bsz, seq) = params
    n, d = x.shape
    in_cols = w_in.shape[-1]
    tm, tn = min(MM_TM, n), MM_TN
    assert GA0 % tn == 0 and d % tn == 0
    gate0 = GA0 // tn

    def in_proj_epilogue(acc, j, o_ref):
        @pl.when(j < gate0)
        def _plain():
            o_ref[...] = acc.astype(o_ref.dtype)

        @pl.when(j >= gate0)
        def _gate():
            o_ref[...] = _sigmoid(acc).astype(o_ref.dtype)

    proj = _matmul(xb, w_in, layer, epilogue=in_proj_epilogue, out_dtype=BF16, name="in_proj")
    proj3 = proj.reshape(bsz, seq, in_cols)

    outs = [_dilated_group(proj3, g, in_cols) for g in range(len(DIL_PAIRS))]
    ya_in = _dil_merge([o for o, _ in outs], [l for _, l in outs])
    yb_in = _swa(proj3, sinks[layer])

    def gate_a_epilogue(acc, j, o_ref, sa_ref):
        o_ref[...] = sa_ref[...].astype(F32) * acc

    ya = _matmul(ya_in, w_proj_a, layer, out_dtype=F32, name="proj_a", epilogue=gate_a_epilogue,
                 extras=[(proj, (tm, tn), lambda j, i: (i, gate0 + j))])

    def gate_b_epilogue(acc, j, o_ref, sb_ref, ya_ref):
        o_ref[...] = (ya_ref[...] + sb_ref[...].astype(F32) * acc).astype(o_ref.dtype)

    mixed_in = _matmul(yb_in, w_proj_b, layer, out_dtype=BF16, name="proj_b", epilogue=gate_b_epilogue,
                       extras=[(proj, (tm, tn), lambda j, i: (i, gate0 + d // tn + j)),
                               (ya, (tm, tn), lambda j, i: (i, j))])

    def residual_epilogue(acc, j, o_ref, x_ref):
        o_ref[...] = alpha * x_ref[...] + acc

    h1 = _matmul(mixed_in, w_o, layer, out_dtype=F32, name="out_proj", epilogue=residual_epilogue,
                 extras=[(x, (tm, tn), lambda j, i: (i, j))])

    x1, x1b, meta, cnt = _ln_router(h1, ln1_g, ln1_b, router_w, router_b, layer)

    n_tiles_max = (n * TOP_K + N_EXPERTS * (EXPERT_TM - 1)) // EXPERT_TM
    pos, _, row_src, items = _route_tables(meta, cnt, n_tiles_max)
    xs = _gather_rows(x1, row_src)
    a = _expert_up(xs, w_mlp1, b_mlp1, layer, items)
    y_sorted = _expert_down(a, w_mlp2, b_mlp2, layer, items)

    ple_dim = w_ple_up.shape[1]

    def ple_epilogue(acc, j, o_ref, p_ref, wup_ref):
        up = jnp.dot(p_ref[...], wup_ref[...].astype(BF16), preferred_element_type=F32)
        o_ref[...] = (_sigmoid(acc) * up).astype(o_ref.dtype)

    ple = _matmul(x1b, w_ple_gate, layer, out_dtype=BF16, name="ple", epilogue=ple_epilogue,
                  extras=[(p_l, (tm, ple_dim), lambda j, i: (i, 0)),
                          (w_ple_up, (None, ple_dim, tn), lambda j, i: (layer, 0, j))])

    return _combine_ln(x1, ple, meta, pos, y_sorted, ln2_g, ln2_b, layer, alpha)


def kernel(x, p, w_in, w_proj_a, w_proj_b, w_o, sinks, ln1_g, ln1_b, router_w, router_b,
           w_mlp1, b_mlp1, w_mlp2, b_mlp2, w_ple_up, w_ple_gate, ln2_g, ln2_b):
    bsz, seq, d = x.shape
    depth = w_in.shape[0]
    alpha = (2 * depth) ** 0.25
    params = (w_in, w_proj_a, w_proj_b, w_o, sinks, ln1_g, ln1_b, router_w, router_b,
              w_mlp1, b_mlp1, w_mlp2, b_mlp2, w_ple_up, w_ple_gate, ln2_g, ln2_b, bsz, seq)
    xf = x.reshape(bsz * seq, d)
    xb = xf.astype(BF16)
    pb = p.reshape(depth, bsz * seq, p.shape[-1]).astype(BF16)
    for layer in range(depth):
        xf, xb = _layer(xf, xb, pb[layer], layer, alpha, params)
    return xf.reshape(bsz, seq, d)
```
